```python
import math
import jax, jax.numpy as jnp
from jax import lax
import numpy as np

D_MODEL = 1024
BATCH = 8
SEQ = 4096
DEPTH = 4

CHUNK = 64
N_MEM = 256
N_MIXERS = 2
SSM_GROUP = 16
SSM_GROUPS = D_MODEL // SSM_GROUP
SSM_STATE = 64
DT_MIN = 1e-3
DT_MAX = 1e-1
CONV_WIDTH = 3
XATTN_HEADS = 4
XATTN_HEAD_DIM = D_MODEL // XATTN_HEADS
MLP_HIDDEN = 4 * D_MODEL
N_SSM_LAYERS = (DEPTH + 1) // 2
N_CONV_LAYERS = DEPTH // 2
NORM_EPS = 1e-6

kernel_name = "hybrid_s5_shortconv_memxattn_trunk"


def _rms_norm(x, g):
    x32 = x.astype(jnp.float32)
    y = x32 * lax.rsqrt(jnp.mean(x32 * x32, axis=-1, keepdims=True) + NORM_EPS)
    return (y * g.astype(jnp.float32)).astype(x.dtype)


def _diag_combine(left, right):
    a1, b1 = left
    a2, b2 = right
    return a1 * a2, a2 * b1 + b2


def _s5_mixer(h, a_re, a_im, log_dt, b_re, b_im, c_re, c_im, d_skip, w_glu):
    bsz, seq, dm = h.shape
    n_chunks = seq // CHUNK
    f32 = jnp.float32
    u = h.astype(f32)
    lam = lax.complex(a_re.astype(f32), a_im.astype(f32))
    dt = jnp.exp(log_dt.astype(f32))[:, None]
    a_bar = jnp.exp(lam * dt)
    b = lax.complex(b_re.astype(f32), b_im.astype(f32))
    b_bar = ((a_bar - 1.0) / lam)[:, :, None] * b
    c = lax.complex(c_re.astype(f32), c_im.astype(f32))
    u_chunks = u.reshape(bsz, n_chunks, CHUNK, SSM_GROUPS, SSM_GROUP).transpose(1, 0, 2, 3, 4)
    a_seq = jnp.broadcast_to(a_bar, (CHUNK, bsz, SSM_GROUPS, SSM_STATE))

    def chunk_step(state, u_c):
        bu = jnp.einsum('bcgh,gph->cbgp', u_c.astype(jnp.complex64), b_bar)
        bu = bu.at[0].add(a_bar * state)
        _, states = lax.associative_scan(_diag_combine, (a_seq, bu), axis=0)
        y_c = jnp.einsum('cbgp,ghp->bcgh', states, c).real
        return states[-1], y_c

    state0 = jnp.zeros((bsz, SSM_GROUPS, SSM_STATE), jnp.complex64)
    _, ys = lax.scan(chunk_step, state0, u_chunks)
    y = ys.transpose(1, 0, 2, 3, 4).reshape(bsz, seq, dm) + d_skip.astype(f32) * u
    y = jax.nn.gelu(y).astype(h.dtype)
    val, gate = jnp.split(y @ w_glu, 2, axis=-1)
    return val * jax.nn.sigmoid(gate)


def _short_conv_mixer(h, w_in, conv_w, w_out):
    dm = h.shape[-1]
    gate_b, gate_c, v = jnp.split(h @ w_in, 3, axis=-1)
    z = gate_c * v
    z = lax.conv_general_dilated(
        z, conv_w[:, None, :].astype(z.dtype), window_strides=(1,),
        padding=[(CONV_WIDTH - 1, 0)], dimension_numbers=('NWC', 'WIO', 'NWC'),
        feature_group_count=dm)
    return (gate_b * z) @ w_out


def _memory_cross_attention(h, mem_n, w_q, w_kv, w_o):
    bsz, seq, dm = h.shape
    n_mem = mem_n.shape[1]
    q = (h @ w_q).reshape(bsz, seq, XATTN_HEADS, XATTN_HEAD_DIM)
    k, v = jnp.split(mem_n @ w_kv, 2, axis=-1)
    k = k.reshape(bsz, n_mem, XATTN_HEADS, XATTN_HEAD_DIM)
    v = v.reshape(bsz, n_mem, XATTN_HEADS, XATTN_HEAD_DIM)
    s = jnp.einsum('bqhd,bkhd->bhqk', q.astype(jnp.float32), k.astype(jnp.float32)) * (XATTN_HEAD_DIM ** -0.5)
    p = jax.nn.softmax(s, axis=-1)
    o = jnp.einsum('bhqk,bkhd->bqhd', p, v.astype(jnp.float32)).reshape(bsz, seq, dm)
    return o.astype(h.dtype) @ w_o


def _sqrelu_mlp(h, w1, w2):
    a = jax.nn.relu(h @ w1)
    return (a * a) @ w2


def setup_inputs(seed: int = 0) -> dict:
    key = jax.random.key(seed)
    ks = jax.random.split(key, 24)
    f32 = jnp.float32
    D, G, P, H = D_MODEL, SSM_GROUPS, SSM_STATE, SSM_GROUP
    Ls, Lc = N_SSM_LAYERS, N_CONV_LAYERS

    def nrm(k, shape, scale):
        return jax.random.normal(k, shape, f32) * scale

    def gain(k, shape):
        return 1.0 + 0.02 * jax.random.normal(k, shape, f32)

    a_im_base = jnp.pi * jnp.arange(P, dtype=f32)
    return {
        "x": jax.random.normal(ks[0], (BATCH, SEQ, D), f32),
        "mem": jax.random.normal(ks[1], (BATCH, N_MEM, D), f32),
        "mem_norm_g": gain(ks[2], (D,)),
        "mix_norm_g": gain(ks[3], (DEPTH, D)),
        "xattn_norm_g": gain(ks[4], (DEPTH, D)),
        "mlp_norm_g": gain(ks[5], (DEPTH, D)),
        "s5_a_re": -0.5 + 0.01 * jax.random.normal(ks[6], (Ls, G, P), f32),
        "s5_a_im": a_im_base + 0.01 * jax.random.normal(ks[7], (Ls, G, P), f32),
        "s5_log_dt": jax.random.uniform(ks[8], (Ls, G), f32, math.log(DT_MIN), math.log(DT_MAX)),
        "s5_b_re": nrm(ks[9], (Ls, G, P, H), (2.0 * H) ** -0.5),
        "s5_b_im": nrm(ks[10], (Ls, G, P, H), (2.0 * H) ** -0.5),
        "s5_c_re": nrm(ks[11], (Ls, G, H, P), P ** -0.5),
        "s5_c_im": nrm(ks[12], (Ls, G, H, P), P ** -0.5),
        "s5_d": nrm(ks[13], (Ls, D), 1.0),
        "s5_w_glu": nrm(ks[14], (Ls, D, 2 * D), D ** -0.5),
        "conv_w_in": nrm(ks[15], (Lc, D, 3 * D), D ** -0.5),
        "conv_w": nrm(ks[16], (Lc, CONV_WIDTH, D), CONV_WIDTH ** -0.5),
        "conv_w_out": nrm(ks[17], (Lc, D, D), D ** -0.5),
        "xa_w_q": nrm(ks[18], (DEPTH, D, D), D ** -0.5),
        "xa_w_kv": nrm(ks[19], (DEPTH, D, 2 * D), D ** -0.5),
        "xa_w_o": nrm(ks[20], (DEPTH, D, D), D ** -0.5),
        "mlp_w1": nrm(ks[21], (DEPTH, D, MLP_HIDDEN), D ** -0.5),
        "mlp_w2": nrm(ks[22], (DEPTH, MLP_HIDDEN, D), MLP_HIDDEN ** -0.5),
        "final_norm_g": gain(ks[23], (D,)),
    }


def reference(x, mem, mem_norm_g, mix_norm_g, xattn_norm_g, mlp_norm_g,
              s5_a_re, s5_a_im, s5_log_dt, s5_b_re, s5_b_im, s5_c_re, s5_c_im,
              s5_d, s5_w_glu, conv_w_in, conv_w, conv_w_out,
              xa_w_q, xa_w_kv, xa_w_o, mlp_w1, mlp_w2, final_norm_g):
    mem_n = _rms_norm(mem, mem_norm_g)
    for i in range(DEPTH):
        h = _rms_norm(x, mix_norm_g[i])
        j = i // N_MIXERS
        if i % N_MIXERS == 0:
            x = x + _s5_mixer(h, s5_a_re[j], s5_a_im[j], s5_log_dt[j], s5_b_re[j], s5_b_im[j],
                              s5_c_re[j], s5_c_im[j], s5_d[j], s5_w_glu[j])
        else:
            x = x + _short_conv_mixer(h, conv_w_in[j], conv_w[j], conv_w_out[j])
        x = x + _memory_cross_attention(_rms_norm(x, xattn_norm_g[i]), mem_n,
                                        xa_w_q[i], xa_w_kv[i], xa_w_o[i])
        x = x + _sqrelu_mlp(_rms_norm(x, mlp_norm_g[i]), mlp_w1[i], mlp_w2[i])
    return _rms_norm(x, final_norm_g)
```

```python
import functools

import jax
import jax.numpy as jnp
from jax import lax
from jax.experimental import pallas as pl
from jax.experimental.pallas import tpu as pltpu

NORM_EPS = 1e-6
XATTN_HEADS = 4
MXU_TILE_V7X = 256
SUBLANES = 8
VMEM_LIMIT_BYTES = 56 * 1024 * 1024

_F32 = jnp.float32
_BF16 = jnp.bfloat16


def _rms(x, g):
    return x * lax.rsqrt(jnp.mean(x * x, axis=-1, keepdims=True) + NORM_EPS) * g


def _params(semantics):
    return pltpu.CompilerParams(dimension_semantics=semantics, vmem_limit_bytes=VMEM_LIMIT_BYTES)


def _const_spec(shape):
    return pl.BlockSpec(shape, lambda *_: (0,) * len(shape), pipeline_mode=pl.Buffered(1))


def _token_spec(t_tile, dm):
    return pl.BlockSpec((None, t_tile, dm), lambda b, i: (b, i, 0))


def _kv_kernel(mem_ref, g_ref, wkv_ref, o_ref):
    m = _rms(mem_ref[...], g_ref[...]).astype(_BF16)
    o_ref[...] = jnp.dot(m, wkv_ref[...], preferred_element_type=_F32).astype(o_ref.dtype)


def _memory_kv(mem, mem_g, w_kv_bf):
    bsz, n_mem, dm = mem.shape
    depth = w_kv_bf.shape[0]
    return pl.pallas_call(
        _kv_kernel,
        grid=(depth, bsz),
        in_specs=[
            pl.BlockSpec((None, n_mem, dm), lambda l, b: (b, 0, 0)),
            pl.BlockSpec((1, dm), lambda l, b: (0, 0)),
            pl.BlockSpec((None, dm, 2 * dm), lambda l, b: (l, 0, 0)),
        ],
        out_specs=pl.BlockSpec((None, None, n_mem, 2 * dm), lambda l, b: (l, b, 0, 0)),
        out_shape=jax.ShapeDtypeStruct((depth, bsz, n_mem, 2 * dm), _BF16),
        compiler_params=_params(("arbitrary", "arbitrary")),
        name="memory_kv",
    )(mem, mem_g.reshape(1, dm), w_kv_bf)


def _s5_discretise(a_re, a_im, log_dt, b_re, b_im, c_re, c_im, batch):
    n_grp, n_state = a_re.shape
    grp = b_re.shape[-1]
    gs = MXU_TILE_V7X // grp
    ns = n_grp // gs
    lam = lax.complex(a_re.astype(_F32), a_im.astype(_F32))
    dt = jnp.exp(log_dt.astype(_F32))[:, None]
    a_bar = jnp.exp(lam * dt)
    b_bar = ((a_bar - 1.0) / lam)[:, :, None] * lax.complex(b_re.astype(_F32), b_im.astype(_F32))
    eye = jnp.eye(gs, dtype=_F32)

    def in_proj(part):
        blk = jnp.einsum("kgph,gj->kghjp", part.reshape(ns, gs, n_state, grp), eye)
        return blk.reshape(ns, gs * grp, gs * n_state)

    def out_proj(part):
        blk = jnp.einsum("kghp,gj->kgpjh", part.reshape(ns, gs, grp, n_state), eye)
        return blk.reshape(ns, gs * n_state, gs * grp)

    wb = jnp.concatenate([in_proj(jnp.real(b_bar)), in_proj(jnp.imag(b_bar))], axis=-1)
    wc = jnp.concatenate([out_proj(c_re.astype(_F32)), out_proj(-c_im.astype(_F32))], axis=-2)
    a_flat = jnp.concatenate(
        [jnp.real(a_bar).reshape(ns, 1, gs * n_state), jnp.imag(a_bar).reshape(ns, 1, gs * n_state)], axis=-1)
    a_rep = jnp.broadcast_to(a_flat, (ns, batch, 2 * gs * n_state))
    return wb.astype(_BF16), a_rep, wc.astype(_BF16)


def _s5_kernel(x_ref, g_ref, wb_ref, a_ref, wc_ref, d_ref, wglu_ref, o_ref, state_ref, bu_ref, y_ref,
               *, n_slab, t_tile, batch):
    @pl.when(pl.program_id(0) == 0)
    def _():
        state_ref[...] = jnp.zeros_like(state_ref)

    x = x_ref[...]
    dm = x.shape[-1]
    u = _rms(x, g_ref[...])
    u_bf = u.astype(_BF16)
    half = wb_ref.shape[-1] // 2
    slab = wb_ref.shape[1]
    for k in range(n_slab):
        bu_ref[...] = jnp.dot(u_bf[:, k * slab:(k + 1) * slab], wb_ref[k], preferred_element_type=_F32)
        a_re = a_ref[k, :, :half]
        a_im = a_ref[k, :, half:]

        def step(t, carry, a_re=a_re, a_im=a_im):
            h_re, h_im = carry
            rows = pl.ds(pl.multiple_of(t * batch, batch), batch)
            n_re = a_re * h_re - a_im * h_im + bu_ref[rows, :half]
            n_im = a_re * h_im + a_im * h_re + bu_ref[rows, half:]
            bu_ref[rows, :half] = n_re
            bu_ref[rows, half:] = n_im
            return n_re, n_im

        h_re, h_im = lax.fori_loop(0, t_tile, step, (state_ref[k, :, :half], state_ref[k, :, half:]), unroll=2)
        state_ref[k, :, :half] = h_re
        state_ref[k, :, half:] = h_im
        y_ref[:, k * slab:(k + 1) * slab] = jnp.dot(
            bu_ref[...].astype(_BF16), wc_ref[k], preferred_element_type=_F32)

    y = y_ref[...] + d_ref[...] * u
    y = jax.nn.gelu(y).astype(_BF16)
    vg = jnp.dot(y, wglu_ref[...], preferred_element_type=_F32)
    o_ref[...] = x + vg[:, :dm] / (1.0 + jnp.exp(-vg[:, dm:]))


def _s5_layer(x, norm_g, wb, a_rep, wc, d_skip, w_glu_bf, t_tile):
    bsz, seq, dm = x.shape
    xt = jnp.transpose(x, (1, 0, 2)).reshape(seq * bsz, dm)
    m_tile = t_tile * bsz
    n_slab, slab, width = wb.shape
    kern = functools.partial(_s5_kernel, n_slab=n_slab, t_tile=t_tile, batch=bsz)
    out = pl.pallas_call(
        kern,
        grid=(seq // t_tile,),
        in_specs=[
            pl.BlockSpec((m_tile, dm), lambda i: (i, 0)),
            _const_spec((1, dm)),
            _const_spec((n_slab, slab, width)),
            _const_spec((n_slab, bsz, width)),
            _const_spec((n_slab, width, slab)),
            _const_spec((1, dm)),
            _const_spec((dm, 2 * dm)),
        ],
        out_specs=pl.BlockSpec((m_tile, dm), lambda i: (i, 0)),
        out_shape=jax.ShapeDtypeStruct((seq * bsz, dm), x.dtype),
        scratch_shapes=[
            pltpu.VMEM((n_slab, bsz, width), _F32),
            pltpu.VMEM((m_tile, width), _F32),
            pltpu.VMEM((m_tile, dm), _F32),
        ],
        compiler_params=_params(("arbitrary",)),
        name="s5_mixer",
    )(xt, norm_g.reshape(1, dm), wb, a_rep, wc, d_skip.reshape(1, dm), w_glu_bf)
    return jnp.transpose(out.reshape(seq, bsz, dm), (1, 0, 2))


def _conv_kernel(x_ref, g_ref, win_ref, cw_ref, wout_ref, o_ref, tail_ref):
    width = cw_ref.shape[0]
    t_tile, dm = x_ref.shape

    @pl.when(pl.program_id(1) == 0)
    def _():
        tail_ref[...] = jnp.zeros_like(tail_ref)

    x = x_ref[...]
    h = _rms(x, g_ref[...]).astype(_BF16)
    proj = jnp.dot(h, win_ref[...], preferred_element_type=_F32)
    z = proj[:, dm:2 * dm] * proj[:, 2 * dm:]
    row = lax.broadcasted_iota(jnp.int32, (t_tile, dm), 0)
    conv = cw_ref[width - 1:width, :] * z
    for lag in range(1, width):
        zl = pltpu.roll(z, lag, 0)
        for r in range(lag):
            zl = jnp.where(row == r, tail_ref[SUBLANES - lag + r:SUBLANES - lag + r + 1, :], zl)
        conv = conv + cw_ref[width - 1 - lag:width - lag, :] * zl
    tail_ref[...] = z[t_tile - SUBLANES:, :]
    gated = (proj[:, :dm] * conv).astype(_BF16)
    o_ref[...] = x + jnp.dot(gated, wout_ref[...], preferred_element_type=_F32)


def _conv_layer(x, norm_g, w_in_bf, conv_w, w_out_bf, t_tile):
    bsz, seq, dm = x.shape
    width = conv_w.shape[0]
    assert width - 1 <= SUBLANES <= t_tile
    return pl.pallas_call(
        _conv_kernel,
        grid=(bsz, seq // t_tile),
        in_specs=[
            _token_spec(t_tile, dm),
            _const_spec((1, dm)),
            _const_spec((dm, 3 * dm)),
            _const_spec((width, dm)),
            _const_spec((dm, dm)),
        ],
        out_specs=_token_spec(t_tile, dm),
        out_shape=jax.ShapeDtypeStruct(x.shape, x.dtype),
        scratch_shapes=[pltpu.VMEM((SUBLANES, dm), _F32)],
        compiler_params=_params(("arbitrary", "arbitrary")),
        name="conv_mixer",
    )(x, norm_g.reshape(1, dm), w_in_bf, conv_w, w_out_bf)


def _xattn_kernel(x_ref, g_ref, wq_ref, k_ref, v_ref, wo_ref, o_ref, *, heads):
    x = x_ref[...]
    dm = x.shape[-1]
    hd = dm // heads
    h = _rms(x, g_ref[...]).astype(_BF16)
    q = jnp.dot(h, wq_ref[...], preferred_element_type=_F32).astype(_BF16)
    outs = []
    for i in range(heads):
        lanes = slice(i * hd, (i + 1) * hd)
        s = lax.dot_general(q[:, lanes], k_ref[:, lanes], (((1,), (1,)), ((), ())),
                            preferred_element_type=_F32) * (hd ** -0.5)
        e = jnp.exp(s - jnp.max(s, axis=-1, keepdims=True))
        p = e * (1.0 / jnp.sum(e, axis=-1, keepdims=True))
        outs.append(jnp.dot(p.astype(_BF16), v_ref[:, lanes], preferred_element_type=_F32))
    o = jnp.concatenate(outs, axis=-1).astype(_BF16)
    o_ref[...] = x + jnp.dot(o, wo_ref[...], preferred_element_type=_F32)


def _xattn_layer(x, norm_g, w_q_bf, kv, layer, w_o_bf, t_tile):
    bsz, seq, dm = x.shape
    n_mem = kv.shape[2]
    kern = functools.partial(_xattn_kernel, heads=XATTN_HEADS)
    return pl.pallas_call(
        kern,
        grid=(bsz, seq // t_tile),
        in_specs=[
            _token_spec(t_tile, dm),
            _const_spec((1, dm)),
            _const_spec((dm, dm)),
            pl.BlockSpec((None, None, n_mem, dm), lambda b, i: (layer, b, 0, 0)),
            pl.BlockSpec((None, None, n_mem, dm), lambda b, i: (layer, b, 0, 1)),
            _const_spec((dm, dm)),
        ],
        out_specs=_token_spec(t_tile, dm),
        out_shape=jax.ShapeDtypeStruct(x.shape, x.dtype),
        compiler_params=_params(("arbitrary", "arbitrary")),
        name="mem_xattn",
    )(x, norm_g.reshape(1, dm), w_q_bf, kv, kv, w_o_bf)


def _mlp_kernel(x_ref, g_ref, w1_ref, w2_ref, o_ref):
    x = x_ref[...]
    h = _rms(x, g_ref[...]).astype(_BF16)
    a = jnp.maximum(jnp.dot(h, w1_ref[...], preferred_element_type=_F32), 0.0)
    a = (a * a).astype(_BF16)
    o_ref[...] = x + jnp.dot(a, w2_ref[...], preferred_element_type=_F32)


def _mlp_final_kernel(x_ref, g_ref, w1_ref, w2_ref, gf_ref, o_ref):
    x = x_ref[...]
    h = _rms(x, g_ref[...]).astype(_BF16)
    a = jnp.maximum(jnp.dot(h, w1_ref[...], preferred_element_type=_F32), 0.0)
    a = (a * a).astype(_BF16)
    o_ref[...] = _rms(x + jnp.dot(a, w2_ref[...], preferred_element_type=_F32), gf_ref[...])


def _mlp_layer(x, norm_g, w1_bf, w2_bf, t_tile, final_g=None):
    bsz, seq, dm = x.shape
    hid = w1_bf.shape[1]
    in_specs = [_token_spec(t_tile, dm), _const_spec((1, dm)), _const_spec((dm, hid)), _const_spec((hid, dm))]
    args = [x, norm_g.reshape(1, dm), w1_bf, w2_bf]
    kern = _mlp_kernel
    if final_g is not None:
        kern = _mlp_final_kernel
        in_specs.append(_const_spec((1, dm)))
        args.append(final_g.reshape(1, dm))
    return pl.pallas_call(
        kern,
        grid=(bsz, seq // t_tile),
        in_specs=in_specs,
        out_specs=_token_spec(t_tile, dm),
        out_shape=jax.ShapeDtypeStruct(x.shape, x.dtype),
        compiler_params=_params(("arbitrary", "arbitrary")),
        name="sqrelu_mlp",
    )(*args)


def _tiles(seq):
    return min(64, seq), min(512, seq)


def kernel(x, mem, mem_norm_g, mix_norm_g, xattn_norm_g, mlp_norm_g, s5_a_re, s5_a_im, s5_log_dt, s5_b_re, s5_b_im, s5_c_re, s5_c_im, s5_d, s5_w_glu, conv_w_in, conv_w, conv_w_out, xa_w_q, xa_w_kv, xa_w_o, mlp_w1, mlp_w2, final_norm_g):
    bsz, seq, dm = x.shape
    depth = mix_norm_g.shape[0]
    assert bsz == SUBLANES, "the S5 scan puts the batch on the sublanes of one tile"
    assert dm % MXU_TILE_V7X == 0
    t_scan, t_row = _tiles(seq)
    assert seq % t_scan == 0 and seq % t_row == 0

    kv = _memory_kv(mem, mem_norm_g, xa_w_kv.astype(_BF16))
    for i in range(depth):
        j = i // 2
        if i % 2 == 0:
            wb, a_rep, wc = _s5_discretise(s5_a_re[j], s5_a_im[j], s5_log_dt[j], s5_b_re[j], s5_b_im[j],
                                           s5_c_re[j], s5_c_im[j], bsz)
            x = _s5_layer(x, mix_norm_g[i], wb, a_rep, wc, s5_d[j], s5_w_glu[j].astype(_BF16), t_scan)
        else:
            x = _conv_layer(x, mix_norm_g[i], conv_w_in[j].astype(_BF16), conv_w[j],
                            conv_w_out[j].astype(_BF16), t_row)
        x = _xattn_layer(x, xattn_norm_g[i], xa_w_q[i].astype(_BF16), kv, i, xa_w_o[i].astype(_BF16), t_row)
        x = _mlp_layer(x, mlp_norm_g[i], mlp_w1[i].astype(_BF16), mlp_w2[i].astype(_BF16), t_row,
                       final_g=final_norm_g if i == depth - 1 else None)
    return x
```

```python
import functools

import jax
import jax.numpy as jnp
from jax import lax
from jax.experimental import pallas as pl
from jax.experimental.pallas import tpu as pltpu

NORM_EPS = 1e-6
XATTN_HEADS = 4
MXU_TILE_V7X = 256
SUBLANES = 8
VMEM_LIMIT_BYTES = 56 * 1024 * 1024

_F32 = jnp.float32
_BF16 = jnp.bfloat16


def _rms(x, g):
    return x * lax.rsqrt(jnp.mean(x * x, axis=-1, keepdims=True) + NORM_EPS) * g


def _params(semantics):
    return pltpu.CompilerParams(dimension_semantics=semantics, vmem_limit_bytes=VMEM_LIMIT_BYTES)


def _const_spec(shape):
    return pl.BlockSpec(shape, lambda *_: (0,) * len(shape), pipeline_mode=pl.Buffered(1))


def _token_spec(t_tile, dm):
    return pl.BlockSpec((None, t_tile, dm), lambda b, i: (b, i, 0))


def _kv_kernel(mem_ref, g_ref, wkv_ref, o_ref):
    m = _rms(mem_ref[...], g_ref[...]).astype(_BF16)
    o_ref[...] = jnp.dot(m, wkv_ref[...], preferred_element_type=_F32).astype(o_ref.dtype)


def _memory_kv(mem, mem_g, w_kv_bf):
    bsz, n_mem, dm = mem.shape
    depth = w_kv_bf.shape[0]
    return pl.pallas_call(
        _kv_kernel,
        grid=(depth, bsz),
        in_specs=[
            pl.BlockSpec((None, n_mem, dm), lambda l, b: (b, 0, 0)),
            pl.BlockSpec((1, dm), lambda l, b: (0, 0)),
            pl.BlockSpec((None, dm, 2 * dm), lambda l, b: (l, 0, 0)),
        ],
        out_specs=pl.BlockSpec((None, None, n_mem, 2 * dm), lambda l, b: (l, b, 0, 0)),
        out_shape=jax.ShapeDtypeStruct((depth, bsz, n_mem, 2 * dm), _BF16),
        compiler_params=_params(("arbitrary", "arbitrary")),
        name="memory_kv",
    )(mem, mem_g.reshape(1, dm), w_kv_bf)


def _s5_discretise(a_re, a_im, log_dt, b_re, b_im, c_re, c_im, batch):
    n_grp, n_state = a_re.shape
    grp = b_re.shape[-1]
    gs = MXU_TILE_V7X // grp
    ns = n_grp // gs
    lam_re, lam_im = a_re.astype(_F32), a_im.astype(_F32)
    dt = jnp.exp(log_dt.astype(_F32))[:, None]
    mag = jnp.exp(lam_re * dt)
    abar_re, abar_im = mag * jnp.cos(lam_im * dt), mag * jnp.sin(lam_im * dt)
    den = lam_re * lam_re + lam_im * lam_im
    q_re = (((abar_re - 1.0) * lam_re + abar_im * lam_im) / den)[:, :, None]
    q_im = ((abar_im * lam_re - (abar_re - 1.0) * lam_im) / den)[:, :, None]
    bbar_re = q_re * b_re.astype(_F32) - q_im * b_im.astype(_F32)
    bbar_im = q_re * b_im.astype(_F32) + q_im * b_re.astype(_F32)
    eye = jnp.eye(gs, dtype=_F32)

    def in_proj(part):
        blk = jnp.einsum("kgph,gj->kghjp", part.reshape(ns, gs, n_state, grp), eye)
        return blk.reshape(ns, gs * grp, gs * n_state)

    def out_proj(part):
        blk = jnp.einsum("kghp,gj->kgpjh", part.reshape(ns, gs, grp, n_state), eye)
        return blk.reshape(ns, gs * n_state, gs * grp)

    wb = jnp.concatenate([in_proj(bbar_re), in_proj(bbar_im)], axis=-1)
    wc = jnp.concatenate([out_proj(c_re.astype(_F32)), out_proj(-c_im.astype(_F32))], axis=-2)
    a_flat = jnp.concatenate(
        [abar_re.reshape(ns, 1, gs * n_state), abar_im.reshape(ns, 1, gs * n_state)], axis=-1)
    a_rep = jnp.broadcast_to(a_flat, (ns, batch, 2 * gs * n_state))
    return wb.astype(_BF16), a_rep, wc.astype(_BF16)


def _s5_kernel(x_ref, g_ref, wb_ref, a_ref, wc_ref, d_ref, wglu_ref, o_ref, state_ref, bu_ref):
    @pl.when(pl.program_id(0) == 0)
    def _():
        state_ref[...] = jnp.zeros_like(state_ref)

    batch, t_tile, dm = x_ref.shape
    n_slab, slab, width = wb_ref.shape
    half = width // 2
    x = jnp.swapaxes(x_ref[...], 0, 1).reshape(t_tile * batch, dm)
    u = _rms(x, g_ref[...])
    u_bf = u.astype(_BF16)
    ys = []
    for k in range(n_slab):
        bu_ref[k] = jnp.dot(u_bf[:, k * slab:(k + 1) * slab], wb_ref[k], preferred_element_type=_F32)
        a_re = a_ref[k, :, :half]
        a_im = a_ref[k, :, half:]
        h_re = state_ref[k, :, :half]
        h_im = state_ref[k, :, half:]
        for t in range(t_tile):
            rows = slice(t * batch, (t + 1) * batch)
            h_re, h_im = (a_re * h_re - a_im * h_im + bu_ref[k, rows, :half],
                          a_re * h_im + a_im * h_re + bu_ref[k, rows, half:])
            bu_ref[k, rows, :half] = h_re
            bu_ref[k, rows, half:] = h_im
        state_ref[k, :, :half] = h_re
        state_ref[k, :, half:] = h_im
        ys.append(jnp.dot(bu_ref[k].astype(_BF16), wc_ref[k], preferred_element_type=_F32))

    y = jnp.concatenate(ys, axis=-1) + d_ref[...] * u
    y = jax.nn.gelu(y).astype(_BF16)
    vg = jnp.dot(y, wglu_ref[...], preferred_element_type=_F32)
    out = x + vg[:, :dm] / (1.0 + jnp.exp(-vg[:, dm:]))
    o_ref[...] = jnp.swapaxes(out.reshape(t_tile, batch, dm), 0, 1)


def _s5_layer(x, norm_g, wb, a_rep, wc, d_skip, w_glu_bf, t_tile):
    bsz, seq, dm = x.shape
    n_slab, slab, width = wb.shape
    tile_spec = pl.BlockSpec((bsz, t_tile, dm), lambda i: (0, i, 0))
    return pl.pallas_call(
        _s5_kernel,
        grid=(seq // t_tile,),
        in_specs=[
            tile_spec,
            _const_spec((1, dm)),
            _const_spec((n_slab, slab, width)),
            _const_spec((n_slab, bsz, width)),
            _const_spec((n_slab, width, slab)),
            _const_spec((1, dm)),
            _const_spec((dm, 2 * dm)),
        ],
        out_specs=tile_spec,
        out_shape=jax.ShapeDtypeStruct(x.shape, x.dtype),
        scratch_shapes=[
            pltpu.VMEM((n_slab, bsz, width), _F32),
            pltpu.VMEM((n_slab, t_tile * bsz, width), _F32),
        ],
        compiler_params=_params(("arbitrary",)),
        name="s5_mixer",
    )(x, norm_g.reshape(1, dm), wb, a_rep, wc, d_skip.reshape(1, dm), w_glu_bf)


def _conv_kernel(x_ref, g_ref, win_ref, cw_ref, wout_ref, o_ref, tail_ref):
    width = cw_ref.shape[0]
    t_tile, dm = x_ref.shape

    @pl.when(pl.program_id(1) == 0)
    def _():
        tail_ref[...] = jnp.zeros_like(tail_ref)

    x = x_ref[...]
    h = _rms(x, g_ref[...]).astype(_BF16)
    proj = jnp.dot(h, win_ref[...], preferred_element_type=_F32)
    z = proj[:, dm:2 * dm] * proj[:, 2 * dm:]
    row = lax.broadcasted_iota(jnp.int32, (t_tile, dm), 0)
    conv = cw_ref[width - 1:width, :] * z
    for lag in range(1, width):
        zl = pltpu.roll(z, lag, 0)
        for r in range(lag):
            zl = jnp.where(row == r, tail_ref[SUBLANES - lag + r:SUBLANES - lag + r + 1, :], zl)
        conv = conv + cw_ref[width - 1 - lag:width - lag, :] * zl
    tail_ref[...] = z[t_tile - SUBLANES:, :]
    gated = (proj[:, :dm] * conv).astype(_BF16)
    o_ref[...] = x + jnp.dot(gated, wout_ref[...], preferred_element_type=_F32)


def _conv_layer(x, norm_g, w_in_bf, conv_w, w_out_bf, t_tile):
    bsz, seq, dm = x.shape
    width = conv_w.shape[0]
    assert width - 1 <= SUBLANES <= t_tile
    return pl.pallas_call(
        _conv_kernel,
        grid=(bsz, seq // t_tile),
        in_specs=[
            _token_spec(t_tile, dm),
            _const_spec((1, dm)),
            _const_spec((dm, 3 * dm)),
            _const_spec((width, dm)),
            _const_spec((dm, dm)),
        ],
        out_specs=_token_spec(t_tile, dm),
        out_shape=jax.ShapeDtypeStruct(x.shape, x.dtype),
        scratch_shapes=[pltpu.VMEM((SUBLANES, dm), _F32)],
        compiler_params=_params(("arbitrary", "arbitrary")),
        name="conv_mixer",
    )(x, norm_g.reshape(1, dm), w_in_bf, conv_w, w_out_bf)


def _xattn_kernel(x_ref, g_ref, wq_ref, k_ref, v_ref, wo_ref, o_ref, *, heads):
    x = x_ref[...]
    dm = x.shape[-1]
    hd = dm // heads
    h = _rms(x, g_ref[...]).astype(_BF16)
    q = jnp.dot(h, wq_ref[...], preferred_element_type=_F32).astype(_BF16)
    outs = []
    for i in range(heads):
        lanes = slice(i * hd, (i + 1) * hd)
        s = lax.dot_general(q[:, lanes], k_ref[:, lanes], (((1,), (1,)), ((), ())),
                            preferred_element_type=_F32) * (hd ** -0.5)
        e = jnp.exp(s - jnp.max(s, axis=-1, keepdims=True))
        p = e * (1.0 / jnp.sum(e, axis=-1, keepdims=True))
        outs.append(jnp.dot(p.astype(_BF16), v_ref[:, lanes], preferred_element_type=_F32))
    o = jnp.concatenate(outs, axis=-1).astype(_BF16)
    o_ref[...] = x + jnp.dot(o, wo_ref[...], preferred_element_type=_F32)


def _xattn_layer(x, norm_g, w_q_bf, kv, layer, w_o_bf, t_tile):
    bsz, seq, dm = x.shape
    n_mem = kv.shape[2]
    kern = functools.partial(_xattn_kernel, heads=XATTN_HEADS)
    return pl.pallas_call(
        kern,
        grid=(bsz, seq // t_tile),
        in_specs=[
            _token_spec(t_tile, dm),
            _const_spec((1, dm)),
            _const_spec((dm, dm)),
            pl.BlockSpec((None, None, n_mem, dm), lambda b, i: (layer, b, 0, 0)),
            pl.BlockSpec((None, None, n_mem, dm), lambda b, i: (layer, b, 0, 1)),
            _const_spec((dm, dm)),
        ],
        out_specs=_token_spec(t_tile, dm),
        out_shape=jax.ShapeDtypeStruct(x.shape, x.dtype),
        compiler_params=_params(("arbitrary", "arbitrary")),
        name="mem_xattn",
    )(x, norm_g.reshape(1, dm), w_q_bf, kv, kv, w_o_bf)


def _mlp_kernel(x_ref, g_ref, w1_ref, w2_ref, o_ref):
    x = x_ref[...]
    h = _rms(x, g_ref[...]).astype(_BF16)
    a = jnp.maximum(jnp.dot(h, w1_ref[...], preferred_element_type=_F32), 0.0)
    a = (a * a).astype(_BF16)
    o_ref[...] = x + jnp.dot(a, w2_ref[...], preferred_element_type=_F32)


def _mlp_final_kernel(x_ref, g_ref, w1_ref, w2_ref, gf_ref, o_ref):
    x = x_ref[...]
    h = _rms(x, g_ref[...]).astype(_BF16)
    a = jnp.maximum(jnp.dot(h, w1_ref[...], preferred_element_type=_F32), 0.0)
    a = (a * a).astype(_BF16)
    o_ref[...] = _rms(x + jnp.dot(a, w2_ref[...], preferred_element_type=_F32), gf_ref[...])


def _mlp_layer(x, norm_g, w1_bf, w2_bf, t_tile, final_g=None):
    bsz, seq, dm = x.shape
    hid = w1_bf.shape[1]
    in_specs = [_token_spec(t_tile, dm), _const_spec((1, dm)), _const_spec((dm, hid)), _const_spec((hid, dm))]
    args = [x, norm_g.reshape(1, dm), w1_bf, w2_bf]
    kern = _mlp_kernel
    if final_g is not None:
        kern = _mlp_final_kernel
        in_specs.append(_const_spec((1, dm)))
        args.append(final_g.reshape(1, dm))
    return pl.pallas_call(
        kern,
        grid=(bsz, seq // t_tile),
        in_specs=in_specs,
        out_specs=_token_spec(t_tile, dm),
        out_shape=jax.ShapeDtypeStruct(x.shape, x.dtype),
        compiler_params=_params(("arbitrary", "arbitrary")),
        name="sqrelu_mlp",
    )(*args)


def _tiles(seq):
    return min(64, seq), min(512, seq)


def kernel(x, mem, mem_norm_g, mix_norm_g, xattn_norm_g, mlp_norm_g, s5_a_re, s5_a_im, s5_log_dt, s5_b_re, s5_b_im, s5_c_re, s5_c_im, s5_d, s5_w_glu, conv_w_in, conv_w, conv_w_out, xa_w_q, xa_w_kv, xa_w_o, mlp_w1, mlp_w2, final_norm_g):
    bsz, seq, dm = x.shape
    depth = mix_norm_g.shape[0]
    assert bsz == SUBLANES, "the S5 scan puts the batch on the sublanes of one tile"
    assert dm % MXU_TILE_V7X == 0
    t_scan, t_row = _tiles(seq)
    assert seq % t_scan == 0 and seq % t_row == 0

    kv = _memory_kv(mem, mem_norm_g, xa_w_kv.astype(_BF16))
    for i in range(depth):
        j = i // 2
        if i % 2 == 0:
            wb, a_rep, wc = _s5_discretise(s5_a_re[j], s5_a_im[j], s5_log_dt[j], s5_b_re[j], s5_b_im[j],
                                           s5_c_re[j], s5_c_im[j], bsz)
            x = _s5_layer(x, mix_norm_g[i], wb, a_rep, wc, s5_d[j], s5_w_glu[j].astype(_BF16), t_scan)
        else:
            x = _conv_layer(x, mix_norm_g[i], conv_w_in[j].astype(_BF16), conv_w[j],
                            conv_w_out[j].astype(_BF16), t_row)
        x = _xattn_layer(x, xattn_norm_g[i], xa_w_q[i].astype(_BF16), kv, i, xa_w_o[i].astype(_BF16), t_row)
        x = _mlp_layer(x, mlp_norm_g[i], mlp_w1[i].astype(_BF16), mlp_w2[i].astype(_BF16), t_row,
                       final_g=final_norm_g if i == depth - 1 else None)
    return x
```

```python
import functools

import jax
import jax.numpy as jnp
from jax import lax
from jax.experimental import pallas as pl
from jax.experimental.pallas import tpu as pltpu

NORM_EPS = 1e-6
XATTN_HEADS = 4
MXU_TILE_V7X = 256
SUBLANES = 8
VMEM_LIMIT_BYTES = 56 * 1024 * 1024

_F32 = jnp.float32
_BF16 = jnp.bfloat16


def _rms(x, g):
    return x * lax.rsqrt(jnp.mean(x * x, axis=-1, keepdims=True) + NORM_EPS) * g


def _params(semantics):
    return pltpu.CompilerParams(dimension_semantics=semantics, vmem_limit_bytes=VMEM_LIMIT_BYTES)


def _layer_spec(stack, layer):
    shape = stack.shape[1:]
    return pl.BlockSpec((None,) + shape, lambda *_: (layer,) + (0,) * len(shape),
                        pipeline_mode=pl.Buffered(1))


def _token_spec(t_tile, dm):
    return pl.BlockSpec((None, t_tile, dm), lambda b, i: (b, i, 0))


def _gains(g):
    return g.reshape(g.shape[0], 1, g.shape[1])


def _kv_kernel(mem_ref, g_ref, wkv_ref, o_ref):
    m = _rms(mem_ref[...], g_ref[...]).astype(_BF16)
    o_ref[...] = jnp.dot(m, wkv_ref[...], preferred_element_type=_F32).astype(o_ref.dtype)


def _memory_kv(mem, mem_g, w_kv_bf):
    bsz, n_mem, dm = mem.shape
    depth = w_kv_bf.shape[0]
    return pl.pallas_call(
        _kv_kernel,
        grid=(depth, bsz),
        in_specs=[
            pl.BlockSpec((None, n_mem, dm), lambda l, b: (b, 0, 0)),
            pl.BlockSpec((1, dm), lambda l, b: (0, 0)),
            pl.BlockSpec((None, dm, 2 * dm), lambda l, b: (l, 0, 0)),
        ],
        out_specs=pl.BlockSpec((None, None, n_mem, 2 * dm), lambda l, b: (l, b, 0, 0)),
        out_shape=jax.ShapeDtypeStruct((depth, bsz, n_mem, 2 * dm), _BF16),
        compiler_params=_params(("arbitrary", "arbitrary")),
        name="memory_kv",
    )(mem, mem_g.reshape(1, dm), w_kv_bf)


def _s5_discretise(a_re, a_im, log_dt, b_re, b_im, c_re, c_im, batch):
    n_lay, n_grp, n_state = a_re.shape
    grp = b_re.shape[-1]
    gs = MXU_TILE_V7X // grp
    ns = n_grp // gs
    lam_re, lam_im = a_re.astype(_F32), a_im.astype(_F32)
    dt = jnp.exp(log_dt.astype(_F32))[..., None]
    mag = jnp.exp(lam_re * dt)
    abar_re, abar_im = mag * jnp.cos(lam_im * dt), mag * jnp.sin(lam_im * dt)
    den = lam_re * lam_re + lam_im * lam_im
    q_re = (((abar_re - 1.0) * lam_re + abar_im * lam_im) / den)[..., None]
    q_im = ((abar_im * lam_re - (abar_re - 1.0) * lam_im) / den)[..., None]
    bbar_re = q_re * b_re.astype(_F32) - q_im * b_im.astype(_F32)
    bbar_im = q_re * b_im.astype(_F32) + q_im * b_re.astype(_F32)
    eye = jnp.eye(gs, dtype=_F32)

    def in_proj(part):
        blk = jnp.einsum("lkgph,gj->lkghjp", part.reshape(n_lay, ns, gs, n_state, grp), eye)
        return blk.reshape(n_lay, ns, gs * grp, gs * n_state)

    def out_proj(part):
        blk = jnp.einsum("lkghp,gj->lkgpjh", part.reshape(n_lay, ns, gs, grp, n_state), eye)
        return blk.reshape(n_lay, ns, gs * n_state, gs * grp)

    wb = jnp.concatenate([in_proj(bbar_re), in_proj(bbar_im)], axis=-1)
    wc = jnp.concatenate([out_proj(c_re.astype(_F32)), out_proj(-c_im.astype(_F32))], axis=-2)
    a_flat = jnp.concatenate([abar_re.reshape(n_lay, ns, 1, gs * n_state),
                              abar_im.reshape(n_lay, ns, 1, gs * n_state)], axis=-1)
    a_rep = jnp.broadcast_to(a_flat, (n_lay, ns, batch, 2 * gs * n_state))
    return wb.astype(_BF16), a_rep, wc.astype(_BF16)


def _s5_kernel(x_ref, g_ref, wb_ref, a_ref, wc_ref, d_ref, wglu_ref, o_ref, state_ref, bu_ref):
    @pl.when(pl.program_id(0) == 0)
    def _():
        state_ref[...] = jnp.zeros_like(state_ref)

    batch, t_tile, dm = x_ref.shape
    n_slab, slab, width = wb_ref.shape
    half = width // 2
    x = jnp.swapaxes(x_ref[...], 0, 1).reshape(t_tile * batch, dm)
    u = _rms(x, g_ref[...])
    u_bf = u.astype(_BF16)
    ys = []
    for k in range(n_slab):
        bu_ref[k] = jnp.dot(u_bf[:, k * slab:(k + 1) * slab], wb_ref[k], preferred_element_type=_F32)
        a_re = a_ref[k, :, :half]
        a_im = a_ref[k, :, half:]
        h_re = state_ref[k, :, :half]
        h_im = state_ref[k, :, half:]
        for t in range(t_tile):
            rows = slice(t * batch, (t + 1) * batch)
            h_re, h_im = (a_re * h_re - a_im * h_im + bu_ref[k, rows, :half],
                          a_re * h_im + a_im * h_re + bu_ref[k, rows, half:])
            bu_ref[k, rows, :half] = h_re
            bu_ref[k, rows, half:] = h_im
        state_ref[k, :, :half] = h_re
        state_ref[k, :, half:] = h_im
        ys.append(jnp.dot(bu_ref[k].astype(_BF16), wc_ref[k], preferred_element_type=_F32))

    y = jnp.concatenate(ys, axis=-1) + d_ref[...] * u
    y = jax.nn.gelu(y).astype(_BF16)
    vg = jnp.dot(y, wglu_ref[...], preferred_element_type=_F32)
    out = x + vg[:, :dm] / (1.0 + jnp.exp(-vg[:, dm:]))
    o_ref[...] = jnp.swapaxes(out.reshape(t_tile, batch, dm), 0, 1)


def _s5_layer(x, gains, layer, wb, a_rep, wc, d_skip, w_glu_bf, j, t_tile):
    bsz, seq, dm = x.shape
    _, n_slab, _, width = wb.shape
    tile_spec = pl.BlockSpec((bsz, t_tile, dm), lambda i: (0, i, 0))
    return pl.pallas_call(
        _s5_kernel,
        grid=(seq // t_tile,),
        in_specs=[
            tile_spec,
            _layer_spec(gains, layer),
            _layer_spec(wb, j),
            _layer_spec(a_rep, j),
            _layer_spec(wc, j),
            _layer_spec(d_skip, j),
            _layer_spec(w_glu_bf, j),
        ],
        out_specs=tile_spec,
        out_shape=jax.ShapeDtypeStruct(x.shape, x.dtype),
        scratch_shapes=[
            pltpu.VMEM((n_slab, bsz, width), _F32),
            pltpu.VMEM((n_slab, t_tile * bsz, width), _F32),
        ],
        compiler_params=_params(("arbitrary",)),
        name="s5_mixer",
    )(x, gains, wb, a_rep, wc, d_skip, w_glu_bf)


def _conv_kernel(x_ref, g_ref, win_ref, cw_ref, wout_ref, o_ref, tail_ref):
    width = cw_ref.shape[0]
    t_tile, dm = x_ref.shape

    @pl.when(pl.program_id(1) == 0)
    def _():
        tail_ref[...] = jnp.zeros_like(tail_ref)

    x = x_ref[...]
    h = _rms(x, g_ref[...]).astype(_BF16)
    proj = jnp.dot(h, win_ref[...], preferred_element_type=_F32)
    z = proj[:, dm:2 * dm] * proj[:, 2 * dm:]
    row = lax.broadcasted_iota(jnp.int32, (t_tile, dm), 0)
    conv = cw_ref[width - 1:width, :] * z
    for lag in range(1, width):
        zl = pltpu.roll(z, lag, 0)
        for r in range(lag):
            zl = jnp.where(row == r, tail_ref[SUBLANES - lag + r:SUBLANES - lag + r + 1, :], zl)
        conv = conv + cw_ref[width - 1 - lag:width - lag, :] * zl
    tail_ref[...] = z[t_tile - SUBLANES:, :]
    gated = (proj[:, :dm] * conv).astype(_BF16)
    o_ref[...] = x + jnp.dot(gated, wout_ref[...], preferred_element_type=_F32)


def _conv_layer(x, gains, layer, w_in_bf, conv_w, w_out_bf, j, t_tile):
    bsz, seq, dm = x.shape
    width = conv_w.shape[1]
    assert width - 1 <= SUBLANES <= t_tile
    return pl.pallas_call(
        _conv_kernel,
        grid=(bsz, seq // t_tile),
        in_specs=[
            _token_spec(t_tile, dm),
            _layer_spec(gains, layer),
            _layer_spec(w_in_bf, j),
            _layer_spec(conv_w, j),
            _layer_spec(w_out_bf, j),
        ],
        out_specs=_token_spec(t_tile, dm),
        out_shape=jax.ShapeDtypeStruct(x.shape, x.dtype),
        scratch_shapes=[pltpu.VMEM((SUBLANES, dm), _F32)],
        compiler_params=_params(("arbitrary", "arbitrary")),
        name="conv_mixer",
    )(x, gains, w_in_bf, conv_w, w_out_bf)


def _xattn_kernel(x_ref, g_ref, wq_ref, k_ref, v_ref, wo_ref, o_ref, *, heads, streams):
    t_tile, dm = x_ref.shape
    hd = dm // heads
    rows_per = t_tile // streams
    for r in range(streams):
        rows = slice(r * rows_per, (r + 1) * rows_per)
        x = x_ref[rows, :]
        h = _rms(x, g_ref[...]).astype(_BF16)
        q = jnp.dot(h, wq_ref[...], preferred_element_type=_F32).astype(_BF16)
        outs = []
        for i in range(heads):
            lanes = slice(i * hd, (i + 1) * hd)
            s = lax.dot_general(q[:, lanes], k_ref[:, lanes], (((1,), (1,)), ((), ())),
                                preferred_element_type=_F32) * (hd ** -0.5)
            e = jnp.exp(s - jnp.max(s, axis=-1, keepdims=True))
            p = e * (1.0 / jnp.sum(e, axis=-1, keepdims=True))
            outs.append(jnp.dot(p.astype(_BF16), v_ref[:, lanes], preferred_element_type=_F32))
        o = jnp.concatenate(outs, axis=-1).astype(_BF16)
        o_ref[rows, :] = x + jnp.dot(o, wo_ref[...], preferred_element_type=_F32)


def _xattn_layer(x, gains, layer, w_q_bf, kv, w_o_bf, t_tile):
    bsz, seq, dm = x.shape
    n_mem = kv.shape[2]
    kern = functools.partial(_xattn_kernel, heads=XATTN_HEADS, streams=2)
    return pl.pallas_call(
        kern,
        grid=(bsz, seq // t_tile),
        in_specs=[
            _token_spec(t_tile, dm),
            _layer_spec(gains, layer),
            _layer_spec(w_q_bf, layer),
            pl.BlockSpec((None, None, n_mem, dm), lambda b, i: (layer, b, 0, 0)),
            pl.BlockSpec((None, None, n_mem, dm), lambda b, i: (layer, b, 0, 1)),
            _layer_spec(w_o_bf, layer),
        ],
        out_specs=_token_spec(t_tile, dm),
        out_shape=jax.ShapeDtypeStruct(x.shape, x.dtype),
        compiler_params=_params(("arbitrary", "arbitrary")),
        name="mem_xattn",
    )(x, gains, w_q_bf, kv, kv, w_o_bf)


def _mlp_kernel(x_ref, g_ref, w1_ref, w2_ref, o_ref):
    x = x_ref[...]
    h = _rms(x, g_ref[...]).astype(_BF16)
    a = jnp.maximum(jnp.dot(h, w1_ref[...], preferred_element_type=_F32), 0.0)
    a = (a * a).astype(_BF16)
    o_ref[...] = x + jnp.dot(a, w2_ref[...], preferred_element_type=_F32)


def _mlp_final_kernel(x_ref, g_ref, w1_ref, w2_ref, gf_ref, o_ref):
    x = x_ref[...]
    h = _rms(x, g_ref[...]).astype(_BF16)
    a = jnp.maximum(jnp.dot(h, w1_ref[...], preferred_element_type=_F32), 0.0)
    a = (a * a).astype(_BF16)
    o_ref[...] = _rms(x + jnp.dot(a, w2_ref[...], preferred_element_type=_F32), gf_ref[...])


def _mlp_layer(x, gains, layer, w1_bf, w2_bf, t_tile, final_g=None):
    bsz, seq, dm = x.shape
    in_specs = [_token_spec(t_tile, dm), _layer_spec(gains, layer), _layer_spec(w1_bf, layer),
                _layer_spec(w2_bf, layer)]
    args = [x, gains, w1_bf, w2_bf]
    kern = _mlp_kernel
    if final_g is not None:
        kern = _mlp_final_kernel
        in_specs.append(pl.BlockSpec((1, dm), lambda b, i: (0, 0)))
        args.append(final_g.reshape(1, dm))
    return pl.pallas_call(
        kern,
        grid=(bsz, seq // t_tile),
        in_specs=in_specs,
        out_specs=_token_spec(t_tile, dm),
        out_shape=jax.ShapeDtypeStruct(x.shape, x.dtype),
        compiler_params=_params(("arbitrary", "arbitrary")),
        name="sqrelu_mlp",
    )(*args)


def _tiles(seq):
    return min(64, seq), min(1024, seq), min(1024, seq), min(512, seq)


def kernel(x, mem, mem_norm_g, mix_norm_g, xattn_norm_g, mlp_norm_g, s5_a_re, s5_a_im, s5_log_dt, s5_b_re, s5_b_im, s5_c_re, s5_c_im, s5_d, s5_w_glu, conv_w_in, conv_w, conv_w_out, xa_w_q, xa_w_kv, xa_w_o, mlp_w1, mlp_w2, final_norm_g):
    bsz, seq, dm = x.shape
    depth = mix_norm_g.shape[0]
    assert bsz == SUBLANES, "the S5 scan puts the batch on the sublanes of one tile"
    assert dm % MXU_TILE_V7X == 0
    t_scan, t_conv, t_attn, t_mlp = _tiles(seq)
    assert all(seq % t == 0 for t in (t_scan, t_conv, t_attn, t_mlp))

    mix_g, xattn_g, mlp_g = _gains(mix_norm_g), _gains(xattn_norm_g), _gains(mlp_norm_g)
    wb, a_rep, wc = _s5_discretise(s5_a_re, s5_a_im, s5_log_dt, s5_b_re, s5_b_im, s5_c_re, s5_c_im, bsz)
    s5_d3, w_glu = _gains(s5_d), s5_w_glu.astype(_BF16)
    w_in, w_out = conv_w_in.astype(_BF16), conv_w_out.astype(_BF16)
    w_q, w_o = xa_w_q.astype(_BF16), xa_w_o.astype(_BF16)
    w1, w2 = mlp_w1.astype(_BF16), mlp_w2.astype(_BF16)

    kv = _memory_kv(mem, mem_norm_g, xa_w_kv.astype(_BF16))
    for i in range(depth):
        j = i // 2
        if i % 2 == 0:
            x = _s5_layer(x, mix_g, i, wb, a_rep, wc, s5_d3, w_glu, j, t_scan)
        else:
            x = _conv_layer(x, mix_g, i, w_in, conv_w, w_out, j, t_conv)
        x = _xattn_layer(x, xattn_g, i, w_q, kv, w_o, t_attn)
        x = _mlp_layer(x, mlp_g, i, w1, w2, t_mlp, final_g=final_norm_g if i == depth - 1 else None)
    return x
```

```python
import functools

import jax
import jax.numpy as jnp
from jax import lax
from jax.experimental import pallas as pl
from jax.experimental.pallas import tpu as pltpu

NORM_EPS = 1e-6
XATTN_HEADS = 4
XATTN_ROW_STREAMS = 2
KV_ROW_TILE = 1024
MXU_TILE_V7X = 256
SUBLANES = 8
VMEM_LIMIT_BYTES = 56 * 1024 * 1024

_F32 = jnp.float32
_BF16 = jnp.bfloat16


def _rms(x, g):
    return x * lax.rsqrt(jnp.mean(x * x, axis=-1, keepdims=True) + NORM_EPS) * g


def _params(semantics):
    return pltpu.CompilerParams(dimension_semantics=semantics, vmem_limit_bytes=VMEM_LIMIT_BYTES)


def _layer_spec(stack, layer):
    shape = stack.shape[1:]
    return pl.BlockSpec((None,) + shape, lambda *_: (layer,) + (0,) * len(shape),
                        pipeline_mode=pl.Buffered(1))


def _token_spec(t_tile, dm):
    return pl.BlockSpec((None, t_tile, dm), lambda b, i: (b, i, 0))


def _gains(g):
    return g.reshape(g.shape[0], 1, g.shape[1])


def _kv_kernel(mem_ref, g_ref, wkv_ref, o_ref):
    m = _rms(mem_ref[...], g_ref[...]).astype(_BF16)
    o_ref[...] = jnp.dot(m, wkv_ref[...], preferred_element_type=_F32).astype(o_ref.dtype)


def _memory_kv(mem, mem_g, w_kv_bf):
    bsz, n_mem, dm = mem.shape
    depth = w_kv_bf.shape[0]
    rows = bsz * n_mem
    r_tile = min(KV_ROW_TILE, rows)
    assert rows % r_tile == 0
    kv = pl.pallas_call(
        _kv_kernel,
        grid=(depth, rows // r_tile),
        in_specs=[
            pl.BlockSpec((r_tile, dm), lambda l, r: (r, 0)),
            pl.BlockSpec((1, dm), lambda l, r: (0, 0)),
            pl.BlockSpec((None, dm, 2 * dm), lambda l, r: (l, 0, 0)),
        ],
        out_specs=pl.BlockSpec((None, r_tile, 2 * dm), lambda l, r: (l, r, 0)),
        out_shape=jax.ShapeDtypeStruct((depth, rows, 2 * dm), _BF16),
        compiler_params=_params(("arbitrary", "arbitrary")),
        name="memory_kv",
    )(mem.reshape(rows, dm), mem_g.reshape(1, dm), w_kv_bf)
    return kv.reshape(depth, bsz, n_mem, 2 * dm)


def _s5_discretise(a_re, a_im, log_dt, b_re, b_im, c_re, c_im, batch):
    n_lay, n_grp, n_state = a_re.shape
    grp = b_re.shape[-1]
    gs = MXU_TILE_V7X // grp
    ns = n_grp // gs
    lam_re, lam_im = a_re.astype(_F32), a_im.astype(_F32)
    dt = jnp.exp(log_dt.astype(_F32))[..., None]
    mag = jnp.exp(lam_re * dt)
    abar_re, abar_im = mag * jnp.cos(lam_im * dt), mag * jnp.sin(lam_im * dt)
    den = lam_re * lam_re + lam_im * lam_im
    q_re = (((abar_re - 1.0) * lam_re + abar_im * lam_im) / den)[..., None]
    q_im = ((abar_im * lam_re - (abar_re - 1.0) * lam_im) / den)[..., None]
    bbar_re = q_re * b_re.astype(_F32) - q_im * b_im.astype(_F32)
    bbar_im = q_re * b_im.astype(_F32) + q_im * b_re.astype(_F32)
    tile, wid = gs * grp, gs * n_state
    same = (jnp.arange(tile) // grp)[:, None] == (jnp.arange(wid) // n_state)[None, :]

    def in_proj(part):
        m = jnp.transpose(part.reshape(n_lay, ns, gs, n_state, grp), (0, 1, 4, 2, 3))
        m = jnp.tile(m.reshape(n_lay, ns, grp, wid), (1, 1, gs, 1))
        return jnp.where(same, m, 0.0).astype(_BF16)

    def out_proj(part):
        m = jnp.transpose(part.reshape(n_lay, ns, gs, grp, n_state), (0, 1, 4, 2, 3))
        m = jnp.tile(m.reshape(n_lay, ns, n_state, tile), (1, 1, gs, 1))
        return jnp.where(same.T, m, 0.0).astype(_BF16)

    wb = jnp.concatenate([in_proj(bbar_re), in_proj(bbar_im)], axis=-1)
    wc = jnp.concatenate([out_proj(c_re.astype(_F32)), out_proj(-c_im.astype(_F32))], axis=-2)
    a_flat = jnp.concatenate([abar_re.reshape(n_lay, ns, 1, wid), abar_im.reshape(n_lay, ns, 1, wid)], axis=-1)
    a_rep = jnp.broadcast_to(a_flat, (n_lay, ns, batch, 2 * wid))
    return wb, a_rep, wc


def _s5_kernel(x_ref, g_ref, wb_ref, a_ref, wc_ref, d_ref, wglu_ref, o_ref, state_ref, bu_ref, h_ref):
    @pl.when(pl.program_id(0) == 0)
    def _():
        state_ref[...] = jnp.zeros_like(state_ref)

    batch, t_tile, dm = x_ref.shape
    n_slab, slab, width = wb_ref.shape
    half = width // 2
    x = jnp.swapaxes(x_ref[...], 0, 1).reshape(t_tile * batch, dm)
    u = _rms(x, g_ref[...])
    u_bf = u.astype(_BF16)
    ys = []
    for k in range(n_slab):
        bu_ref[k] = jnp.dot(u_bf[:, k * slab:(k + 1) * slab], wb_ref[k], preferred_element_type=_F32)
        a_re = a_ref[k, :, :half]
        a_im = a_ref[k, :, half:]
        h_re = state_ref[k, :, :half]
        h_im = state_ref[k, :, half:]
        for t in range(0, t_tile, 2):
            pair = []
            for s in (t, t + 1):
                rows = slice(s * batch, (s + 1) * batch)
                h_re, h_im = (a_re * h_re - a_im * h_im + bu_ref[k, rows, :half],
                              a_re * h_im + a_im * h_re + bu_ref[k, rows, half:])
                pair.append((h_re, h_im))
            rows2 = slice(t * batch, (t + 2) * batch)
            h_ref[k, rows2, :half] = jnp.concatenate([pair[0][0], pair[1][0]], axis=0).astype(_BF16)
            h_ref[k, rows2, half:] = jnp.concatenate([pair[0][1], pair[1][1]], axis=0).astype(_BF16)
        state_ref[k, :, :half] = h_re
        state_ref[k, :, half:] = h_im
        ys.append(jnp.dot(h_ref[k], wc_ref[k], preferred_element_type=_F32))

    y = jnp.concatenate(ys, axis=-1) + d_ref[...] * u
    y = jax.nn.gelu(y).astype(_BF16)
    vg = jnp.dot(y, wglu_ref[...], preferred_element_type=_F32)
    out = x + vg[:, :dm] / (1.0 + jnp.exp(-vg[:, dm:]))
    o_ref[...] = jnp.swapaxes(out.reshape(t_tile, batch, dm), 0, 1)


def _s5_layer(x, gains, layer, wb, a_rep, wc, d_skip, w_glu_bf, j, t_tile):
    bsz, seq, dm = x.shape
    _, n_slab, _, width = wb.shape
    tile_spec = pl.BlockSpec((bsz, t_tile, dm), lambda i: (0, i, 0))
    return pl.pallas_call(
        _s5_kernel,
        grid=(seq // t_tile,),
        in_specs=[
            tile_spec,
            _layer_spec(gains, layer),
            _layer_spec(wb, j),
            _layer_spec(a_rep, j),
            _layer_spec(wc, j),
            _layer_spec(d_skip, j),
            _layer_spec(w_glu_bf, j),
        ],
        out_specs=tile_spec,
        out_shape=jax.ShapeDtypeStruct(x.shape, x.dtype),
        scratch_shapes=[
            pltpu.VMEM((n_slab, bsz, width), _F32),
            pltpu.VMEM((n_slab, t_tile * bsz, width), _F32),
            pltpu.VMEM((n_slab, t_tile * bsz, width), _BF16),
        ],
        compiler_params=_params(("arbitrary",)),
        name="s5_mixer",
    )(x, gains, wb, a_rep, wc, d_skip, w_glu_bf)


def _conv_kernel(x_ref, g_ref, win_ref, cw_ref, wout_ref, o_ref, tail_ref):
    width = cw_ref.shape[0]
    t_tile, dm = x_ref.shape

    @pl.when(pl.program_id(1) == 0)
    def _():
        tail_ref[...] = jnp.zeros_like(tail_ref)

    x = x_ref[...]
    h = _rms(x, g_ref[...]).astype(_BF16)
    proj = jnp.dot(h, win_ref[...], preferred_element_type=_F32)
    z = proj[:, dm:2 * dm] * proj[:, 2 * dm:]
    row = lax.broadcasted_iota(jnp.int32, (t_tile, dm), 0)
    conv = cw_ref[width - 1:width, :] * z
    for lag in range(1, width):
        zl = pltpu.roll(z, lag, 0)
        for r in range(lag):
            zl = jnp.where(row == r, tail_ref[SUBLANES - lag + r:SUBLANES - lag + r + 1, :], zl)
        conv = conv + cw_ref[width - 1 - lag:width - lag, :] * zl
    tail_ref[...] = z[t_tile - SUBLANES:, :]
    gated = (proj[:, :dm] * conv).astype(_BF16)
    o_ref[...] = x + jnp.dot(gated, wout_ref[...], preferred_element_type=_F32)


def _conv_layer(x, gains, layer, w_in_bf, conv_w, w_out_bf, j, t_tile):
    bsz, seq, dm = x.shape
    width = conv_w.shape[1]
    assert width - 1 <= SUBLANES <= t_tile
    return pl.pallas_call(
        _conv_kernel,
        grid=(bsz, seq // t_tile),
        in_specs=[
            _token_spec(t_tile, dm),
            _layer_spec(gains, layer),
            _layer_spec(w_in_bf, j),
            _layer_spec(conv_w, j),
            _layer_spec(w_out_bf, j),
        ],
        out_specs=_token_spec(t_tile, dm),
        out_shape=jax.ShapeDtypeStruct(x.shape, x.dtype),
        scratch_shapes=[pltpu.VMEM((SUBLANES, dm), _F32)],
        compiler_params=_params(("arbitrary", "arbitrary")),
        name="conv_mixer",
    )(x, gains, w_in_bf, conv_w, w_out_bf)


def _xattn_kernel(x_ref, g_ref, wq_ref, k_ref, v_ref, wo_ref, o_ref, *, heads, streams):
    t_tile, dm = x_ref.shape
    hd = dm // heads
    rows_per = t_tile // streams
    for r in range(streams):
        rows = slice(r * rows_per, (r + 1) * rows_per)
        x = x_ref[rows, :]
        h = _rms(x, g_ref[...]).astype(_BF16)
        q = jnp.dot(h, wq_ref[...], preferred_element_type=_F32).astype(_BF16)
        outs = []
        for i in range(heads):
            lanes = slice(i * hd, (i + 1) * hd)
            s = lax.dot_general(q[:, lanes], k_ref[:, lanes], (((1,), (1,)), ((), ())),
                                preferred_element_type=_F32) * (hd ** -0.5)
            e = jnp.exp(s - jnp.max(s, axis=-1, keepdims=True))
            p = e * (1.0 / jnp.sum(e, axis=-1, keepdims=True))
            outs.append(jnp.dot(p.astype(_BF16), v_ref[:, lanes], preferred_element_type=_F32))
        o = jnp.concatenate(outs, axis=-1).astype(_BF16)
        o_ref[rows, :] = x + jnp.dot(o, wo_ref[...], preferred_element_type=_F32)


def _xattn_layer(x, gains, layer, w_q_bf, kv, w_o_bf, t_tile):
    bsz, seq, dm = x.shape
    n_mem = kv.shape[2]
    kern = functools.partial(_xattn_kernel, heads=XATTN_HEADS, streams=XATTN_ROW_STREAMS)
    return pl.pallas_call(
        kern,
        grid=(bsz, seq // t_tile),
        in_specs=[
            _token_spec(t_tile, dm),
            _layer_spec(gains, layer),
            _layer_spec(w_q_bf, layer),
            pl.BlockSpec((None, None, n_mem, dm), lambda b, i: (layer, b, 0, 0)),
            pl.BlockSpec((None, None, n_mem, dm), lambda b, i: (layer, b, 0, 1)),
            _layer_spec(w_o_bf, layer),
        ],
        out_specs=_token_spec(t_tile, dm),
        out_shape=jax.ShapeDtypeStruct(x.shape, x.dtype),
        compiler_params=_params(("arbitrary", "arbitrary")),
        name="mem_xattn",
    )(x, gains, w_q_bf, kv, kv, w_o_bf)


def _mlp_kernel(x_ref, g_ref, w1_ref, w2_ref, o_ref):
    x = x_ref[...]
    h = _rms(x, g_ref[...]).astype(_BF16)
    a = jnp.maximum(jnp.dot(h, w1_ref[...], preferred_element_type=_F32), 0.0)
    a = (a * a).astype(_BF16)
    o_ref[...] = x + jnp.dot(a, w2_ref[...], preferred_element_type=_F32)


def _mlp_final_kernel(x_ref, g_ref, w1_ref, w2_ref, gf_ref, o_ref):
    x = x_ref[...]
    h = _rms(x, g_ref[...]).astype(_BF16)
    a = jnp.maximum(jnp.dot(h, w1_ref[...], preferred_element_type=_F32), 0.0)
    a = (a * a).astype(_BF16)
    o_ref[...] = _rms(x + jnp.dot(a, w2_ref[...], preferred_element_type=_F32), gf_ref[...])


def _mlp_layer(x, gains, layer, w1_bf, w2_bf, t_tile, final_g=None):
    bsz, seq, dm = x.shape
    in_specs = [_token_spec(t_tile, dm), _layer_spec(gains, layer), _layer_spec(w1_bf, layer),
                _layer_spec(w2_bf, layer)]
    args = [x, gains, w1_bf, w2_bf]
    kern = _mlp_kernel
    if final_g is not None:
        kern = _mlp_final_kernel
        in_specs.append(pl.BlockSpec((1, dm), lambda b, i: (0, 0)))
        args.append(final_g.reshape(1, dm))
    return pl.pallas_call(
        kern,
        grid=(bsz, seq // t_tile),
        in_specs=in_specs,
        out_specs=_token_spec(t_tile, dm),
        out_shape=jax.ShapeDtypeStruct(x.shape, x.dtype),
        compiler_params=_params(("arbitrary", "arbitrary")),
        name="sqrelu_mlp",
    )(*args)


def _tiles(seq):
    return min(64, seq), min(1024, seq), min(2048, seq), min(512, seq)


def kernel(x, mem, mem_norm_g, mix_norm_g, xattn_norm_g, mlp_norm_g, s5_a_re, s5_a_im, s5_log_dt, s5_b_re, s5_b_im, s5_c_re, s5_c_im, s5_d, s5_w_glu, conv_w_in, conv_w, conv_w_out, xa_w_q, xa_w_kv, xa_w_o, mlp_w1, mlp_w2, final_norm_g):
    bsz, seq, dm = x.shape
    depth = mix_norm_g.shape[0]
    assert bsz == SUBLANES, "the S5 scan puts the batch on the sublanes of one tile"
    assert dm % MXU_TILE_V7X == 0
    t_scan, t_conv, t_attn, t_mlp = _tiles(seq)
    assert all(seq % t == 0 for t in (t_scan, t_conv, t_attn, t_mlp))

    mix_g, xattn_g, mlp_g = _gains(mix_norm_g), _gains(xattn_norm_g), _gains(mlp_norm_g)
    wb, a_rep, wc = _s5_discretise(s5_a_re, s5_a_im, s5_log_dt, s5_b_re, s5_b_im, s5_c_re, s5_c_im, bsz)
    s5_d3, w_glu = _gains(s5_d), s5_w_glu.astype(_BF16)
    w_in, w_out = conv_w_in.astype(_BF16), conv_w_out.astype(_BF16)
    w_q, w_o = xa_w_q.astype(_BF16), xa_w_o.astype(_BF16)
    w1, w2 = mlp_w1.astype(_BF16), mlp_w2.astype(_BF16)

    kv = _memory_kv(mem, mem_norm_g, xa_w_kv.astype(_BF16))
    for i in range(depth):
        j = i // 2
        if i % 2 == 0:
            x = _s5_layer(x, mix_g, i, wb, a_rep, wc, s5_d3, w_glu, j, t_scan)
        else:
            x = _conv_layer(x, mix_g, i, w_in, conv_w, w_out, j, t_conv)
        x = _xattn_layer(x, xattn_g, i, w_q, kv, w_o, t_attn)
        x = _mlp_layer(x, mlp_g, i, w1, w2, t_mlp, final_g=final_norm_g if i == depth - 1 else None)
    return x
```

```python
import functools

import jax
import jax.numpy as jnp
from jax import lax
from jax.experimental import pallas as pl
from jax.experimental.pallas import tpu as pltpu

NORM_EPS = 1e-6
XATTN_HEADS = 4
XATTN_ROW_STREAMS = 2
KV_ROW_TILE = 1024
S5_TIME_PARTS = 2
MXU_TILE_V7X = 256
SUBLANES = 8
VMEM_LIMIT_BYTES = 56 * 1024 * 1024

_F32 = jnp.float32
_BF16 = jnp.bfloat16


def _rms(x, g):
    return x * lax.rsqrt(jnp.mean(x * x, axis=-1, keepdims=True) + NORM_EPS) * g


def _params(semantics):
    return pltpu.CompilerParams(dimension_semantics=semantics, vmem_limit_bytes=VMEM_LIMIT_BYTES)


def _layer_spec(stack, layer):
    shape = stack.shape[1:]
    return pl.BlockSpec((None,) + shape, lambda *_: (layer,) + (0,) * len(shape),
                        pipeline_mode=pl.Buffered(1))


def _token_spec(t_tile, dm):
    return pl.BlockSpec((None, t_tile, dm), lambda b, i: (b, i, 0))


def _gains(g):
    return g.reshape(g.shape[0], 1, g.shape[1])


def _kv_kernel(mem_ref, g_ref, wkv_ref, o_ref):
    m = _rms(mem_ref[...], g_ref[...]).astype(_BF16)
    o_ref[...] = jnp.dot(m, wkv_ref[...], preferred_element_type=_F32).astype(o_ref.dtype)


def _memory_kv(mem, mem_g, w_kv_bf):
    bsz, n_mem, dm = mem.shape
    depth = w_kv_bf.shape[0]
    rows = bsz * n_mem
    r_tile = min(KV_ROW_TILE, rows)
    assert rows % r_tile == 0
    kv = pl.pallas_call(
        _kv_kernel,
        grid=(depth, rows // r_tile),
        in_specs=[
            pl.BlockSpec((r_tile, dm), lambda l, r: (r, 0)),
            pl.BlockSpec((1, dm), lambda l, r: (0, 0)),
            pl.BlockSpec((None, dm, 2 * dm), lambda l, r: (l, 0, 0)),
        ],
        out_specs=pl.BlockSpec((None, r_tile, 2 * dm), lambda l, r: (l, r, 0)),
        out_shape=jax.ShapeDtypeStruct((depth, rows, 2 * dm), _BF16),
        compiler_params=_params(("arbitrary", "arbitrary")),
        name="memory_kv",
    )(mem.reshape(rows, dm), mem_g.reshape(1, dm), w_kv_bf)
    return kv.reshape(depth, bsz, n_mem, 2 * dm)


def _s5_discretise(a_re, a_im, log_dt, b_re, b_im, c_re, c_im, batch):
    n_lay, n_grp, n_state = a_re.shape
    grp = b_re.shape[-1]
    gs = MXU_TILE_V7X // grp
    ns = n_grp // gs
    lam_re, lam_im = a_re.astype(_F32), a_im.astype(_F32)
    dt = jnp.exp(log_dt.astype(_F32))[..., None]
    mag = jnp.exp(lam_re * dt)
    abar_re, abar_im = mag * jnp.cos(lam_im * dt), mag * jnp.sin(lam_im * dt)
    den = lam_re * lam_re + lam_im * lam_im
    q_re = (((abar_re - 1.0) * lam_re + abar_im * lam_im) / den)[..., None]
    q_im = ((abar_im * lam_re - (abar_re - 1.0) * lam_im) / den)[..., None]
    bbar_re = q_re * b_re.astype(_F32) - q_im * b_im.astype(_F32)
    bbar_im = q_re * b_im.astype(_F32) + q_im * b_re.astype(_F32)
    tile, wid = gs * grp, gs * n_state
    same = (jnp.arange(tile) // grp)[:, None] == (jnp.arange(wid) // n_state)[None, :]

    def in_proj(part):
        m = jnp.transpose(part.reshape(n_lay, ns, gs, n_state, grp), (0, 1, 4, 2, 3))
        m = jnp.tile(m.reshape(n_lay, ns, grp, wid), (1, 1, gs, 1))
        return jnp.where(same, m, 0.0).astype(_BF16)

    def out_proj(part):
        m = jnp.transpose(part.reshape(n_lay, ns, gs, grp, n_state), (0, 1, 4, 2, 3))
        m = jnp.tile(m.reshape(n_lay, ns, n_state, tile), (1, 1, gs, 1))
        return jnp.where(same.T, m, 0.0).astype(_BF16)

    wb = jnp.concatenate([in_proj(bbar_re), in_proj(bbar_im)], axis=-1)
    wc = jnp.concatenate([out_proj(c_re.astype(_F32)), out_proj(-c_im.astype(_F32))], axis=-2)
    a_flat = jnp.concatenate([abar_re.reshape(n_lay, ns, 1, wid), abar_im.reshape(n_lay, ns, 1, wid)], axis=-1)
    a_rep = jnp.broadcast_to(a_flat, (n_lay, ns, batch, 2 * wid))
    return wb, a_rep, wc


def _s5_kernel(x_ref, g_ref, wb_ref, a_ref, wc_ref, d_ref, wglu_ref, o_ref, state_ref, bu_ref, h_ref):
    @pl.when(pl.program_id(0) == 0)
    def _():
        state_ref[...] = jnp.zeros_like(state_ref)

    batch, t_tile, dm = x_ref.shape
    n_slab, slab, width = wb_ref.shape
    half = width // 2
    t_part = t_tile // S5_TIME_PARTS
    m_part = t_part * batch
    xs, us, ubfs = [], [], []
    for p in range(S5_TIME_PARTS):
        xp = jnp.swapaxes(x_ref[:, p * t_part:(p + 1) * t_part, :], 0, 1).reshape(m_part, dm)
        up = _rms(xp, g_ref[...])
        xs.append(xp)
        us.append(up)
        ubfs.append(up.astype(_BF16))
    ys = [[] for _ in range(S5_TIME_PARTS)]
    for k in range(n_slab):
        for p in range(S5_TIME_PARTS):
            bu_ref[k, p * m_part:(p + 1) * m_part, :] = jnp.dot(
                ubfs[p][:, k * slab:(k + 1) * slab], wb_ref[k], preferred_element_type=_F32)
        a_re = a_ref[k, :, :half]
        a_im = a_ref[k, :, half:]
        h_re = state_ref[k, :, :half]
        h_im = state_ref[k, :, half:]
        for t in range(0, t_tile, 2):
            pair = []
            for s in (t, t + 1):
                rows = slice(s * batch, (s + 1) * batch)
                h_re, h_im = (a_re * h_re - a_im * h_im + bu_ref[k, rows, :half],
                              a_re * h_im + a_im * h_re + bu_ref[k, rows, half:])
                pair.append((h_re, h_im))
            rows2 = slice(t * batch, (t + 2) * batch)
            h_ref[k, rows2, :half] = jnp.concatenate([pair[0][0], pair[1][0]], axis=0).astype(_BF16)
            h_ref[k, rows2, half:] = jnp.concatenate([pair[0][1], pair[1][1]], axis=0).astype(_BF16)
        state_ref[k, :, :half] = h_re
        state_ref[k, :, half:] = h_im
        for p in range(S5_TIME_PARTS):
            ys[p].append(jnp.dot(h_ref[k, p * m_part:(p + 1) * m_part, :], wc_ref[k],
                                 preferred_element_type=_F32))

    for p in range(S5_TIME_PARTS):
        y = jnp.concatenate(ys[p], axis=-1) + d_ref[...] * us[p]
        y = jax.nn.gelu(y).astype(_BF16)
        vg = jnp.dot(y, wglu_ref[...], preferred_element_type=_F32)
        out = xs[p] + vg[:, :dm] / (1.0 + jnp.exp(-vg[:, dm:]))
        o_ref[:, p * t_part:(p + 1) * t_part, :] = jnp.swapaxes(out.reshape(t_part, batch, dm), 0, 1)


def _s5_layer(x, gains, layer, wb, a_rep, wc, d_skip, w_glu_bf, j, t_tile):
    bsz, seq, dm = x.shape
    _, n_slab, _, width = wb.shape
    tile_spec = pl.BlockSpec((bsz, t_tile, dm), lambda i: (0, i, 0))
    return pl.pallas_call(
        _s5_kernel,
        grid=(seq // t_tile,),
        in_specs=[
            tile_spec,
            _layer_spec(gains, layer),
            _layer_spec(wb, j),
            _layer_spec(a_rep, j),
            _layer_spec(wc, j),
            _layer_spec(d_skip, j),
            _layer_spec(w_glu_bf, j),
        ],
        out_specs=tile_spec,
        out_shape=jax.ShapeDtypeStruct(x.shape, x.dtype),
        scratch_shapes=[
            pltpu.VMEM((n_slab, bsz, width), _F32),
            pltpu.VMEM((n_slab, t_tile * bsz, width), _F32),
            pltpu.VMEM((n_slab, t_tile * bsz, width), _BF16),
        ],
        compiler_params=_params(("arbitrary",)),
        name="s5_mixer",
    )(x, gains, wb, a_rep, wc, d_skip, w_glu_bf)


def _conv_kernel(x_ref, g_ref, win_ref, cw_ref, wout_ref, o_ref, tail_ref):
    width = cw_ref.shape[0]
    t_tile, dm = x_ref.shape

    @pl.when(pl.program_id(1) == 0)
    def _():
        tail_ref[...] = jnp.zeros_like(tail_ref)

    x = x_ref[...]
    h = _rms(x, g_ref[...]).astype(_BF16)
    proj = jnp.dot(h, win_ref[...], preferred_element_type=_F32)
    z = proj[:, dm:2 * dm] * proj[:, 2 * dm:]
    row = lax.broadcasted_iota(jnp.int32, (t_tile, dm), 0)
    conv = cw_ref[width - 1:width, :] * z
    for lag in range(1, width):
        zl = pltpu.roll(z, lag, 0)
        for r in range(lag):
            zl = jnp.where(row == r, tail_ref[SUBLANES - lag + r:SUBLANES - lag + r + 1, :], zl)
        conv = conv + cw_ref[width - 1 - lag:width - lag, :] * zl
    tail_ref[...] = z[t_tile - SUBLANES:, :]
    gated = (proj[:, :dm] * conv).astype(_BF16)
    o_ref[...] = x + jnp.dot(gated, wout_ref[...], preferred_element_type=_F32)


def _conv_layer(x, gains, layer, w_in_bf, conv_w, w_out_bf, j, t_tile):
    bsz, seq, dm = x.shape
    width = conv_w.shape[1]
    assert width - 1 <= SUBLANES <= t_tile
    return pl.pallas_call(
        _conv_kernel,
        grid=(bsz, seq // t_tile),
        in_specs=[
            _token_spec(t_tile, dm),
            _layer_spec(gains, layer),
            _layer_spec(w_in_bf, j),
            _layer_spec(conv_w, j),
            _layer_spec(w_out_bf, j),
        ],
        out_specs=_token_spec(t_tile, dm),
        out_shape=jax.ShapeDtypeStruct(x.shape, x.dtype),
        scratch_shapes=[pltpu.VMEM((SUBLANES, dm), _F32)],
        compiler_params=_params(("arbitrary", "arbitrary")),
        name="conv_mixer",
    )(x, gains, w_in_bf, conv_w, w_out_bf)


def _fold_kernel(k_ref, v_ref, wq_ref, wo_ref, qk_ref, vo_ref, *, heads):
    n_mem, dm = k_ref.shape
    hd = dm // heads
    for i in range(heads):
        lanes = slice(i * hd, (i + 1) * hd)
        keys = slice(i * n_mem, (i + 1) * n_mem)
        qk_ref[:, keys] = lax.dot_general(wq_ref[:, lanes], k_ref[:, lanes], (((1,), (1,)), ((), ())),
                                          preferred_element_type=_F32).astype(qk_ref.dtype)
        vo_ref[keys, :] = jnp.dot(v_ref[:, lanes], wo_ref[lanes, :],
                                  preferred_element_type=_F32).astype(vo_ref.dtype)


def _fold_memory(kv, w_q_bf, w_o_bf):
    depth, bsz, n_mem, dm2 = kv.shape
    dm = dm2 // 2
    hk = XATTN_HEADS * n_mem
    w_spec = pl.BlockSpec((None, dm, dm), lambda l, b: (l, 0, 0))
    return pl.pallas_call(
        functools.partial(_fold_kernel, heads=XATTN_HEADS),
        grid=(depth, bsz),
        in_specs=[
            pl.BlockSpec((None, None, n_mem, dm), lambda l, b: (l, b, 0, 0)),
            pl.BlockSpec((None, None, n_mem, dm), lambda l, b: (l, b, 0, 1)),
            w_spec,
            w_spec,
        ],
        out_specs=[
            pl.BlockSpec((None, None, dm, hk), lambda l, b: (l, b, 0, 0)),
            pl.BlockSpec((None, None, hk, dm), lambda l, b: (l, b, 0, 0)),
        ],
        out_shape=[
            jax.ShapeDtypeStruct((depth, bsz, dm, hk), _BF16),
            jax.ShapeDtypeStruct((depth, bsz, hk, dm), _BF16),
        ],
        compiler_params=_params(("arbitrary", "arbitrary")),
        name="fold_memory",
    )(kv, kv, w_q_bf, w_o_bf)


def _xattn_kernel(x_ref, g_ref, qk_ref, vo_ref, o_ref, *, heads, streams):
    t_tile, dm = x_ref.shape
    n_mem = qk_ref.shape[1] // heads
    scale = (dm // heads) ** -0.5
    rows_per = t_tile // streams
    for r in range(streams):
        rows = slice(r * rows_per, (r + 1) * rows_per)
        x = x_ref[rows, :]
        h = _rms(x, g_ref[...]).astype(_BF16)
        s_all = jnp.dot(h, qk_ref[...], preferred_element_type=_F32) * scale
        ps = []
        for i in range(heads):
            s = s_all[:, i * n_mem:(i + 1) * n_mem]
            e = jnp.exp(s - jnp.max(s, axis=-1, keepdims=True))
            ps.append((e * (1.0 / jnp.sum(e, axis=-1, keepdims=True))).astype(_BF16))
        p = jnp.concatenate(ps, axis=-1)
        o_ref[rows, :] = x + jnp.dot(p, vo_ref[...], preferred_element_type=_F32)


def _xattn_layer(x, gains, layer, qk, vo, t_tile):
    bsz, seq, dm = x.shape
    hk = qk.shape[-1]
    kern = functools.partial(_xattn_kernel, heads=XATTN_HEADS, streams=XATTN_ROW_STREAMS)
    return pl.pallas_call(
        kern,
        grid=(bsz, seq // t_tile),
        in_specs=[
            _token_spec(t_tile, dm),
            _layer_spec(gains, layer),
            pl.BlockSpec((None, None, dm, hk), lambda b, i: (layer, b, 0, 0)),
            pl.BlockSpec((None, None, hk, dm), lambda b, i: (layer, b, 0, 0)),
        ],
        out_specs=_token_spec(t_tile, dm),
        out_shape=jax.ShapeDtypeStruct(x.shape, x.dtype),
        compiler_params=_params(("arbitrary", "arbitrary")),
        name="mem_xattn",
    )(x, gains, qk, vo)


def _mlp_kernel(x_ref, g_ref, w1_ref, w2_ref, o_ref):
    x = x_ref[...]
    h = _rms(x, g_ref[...]).astype(_BF16)
    a = jnp.maximum(jnp.dot(h, w1_ref[...], preferred_element_type=_F32), 0.0)
    a = (a * a).astype(_BF16)
    o_ref[...] = x + jnp.dot(a, w2_ref[...], preferred_element_type=_F32)


def _mlp_final_kernel(x_ref, g_ref, w1_ref, w2_ref, gf_ref, o_ref):
    x = x_ref[...]
    h = _rms(x, g_ref[...]).astype(_BF16)
    a = jnp.maximum(jnp.dot(h, w1_ref[...], preferred_element_type=_F32), 0.0)
    a = (a * a).astype(_BF16)
    o_ref[...] = _rms(x + jnp.dot(a, w2_ref[...], preferred_element_type=_F32), gf_ref[...])


def _mlp_layer(x, gains, layer, w1_bf, w2_bf, t_tile, final_g=None):
    bsz, seq, dm = x.shape
    in_specs = [_token_spec(t_tile, dm), _layer_spec(gains, layer), _layer_spec(w1_bf, layer),
                _layer_spec(w2_bf, layer)]
    args = [x, gains, w1_bf, w2_bf]
    kern = _mlp_kernel
    if final_g is not None:
        kern = _mlp_final_kernel
        in_specs.append(pl.BlockSpec((1, dm), lambda b, i: (0, 0)))
        args.append(final_g.reshape(1, dm))
    return pl.pallas_call(
        kern,
        grid=(bsz, seq // t_tile),
        in_specs=in_specs,
        out_specs=_token_spec(t_tile, dm),
        out_shape=jax.ShapeDtypeStruct(x.shape, x.dtype),
        compiler_params=_params(("arbitrary", "arbitrary")),
        name="sqrelu_mlp",
    )(*args)


def _tiles(seq):
    return min(64, seq), min(1024, seq), min(2048, seq), min(512, seq)


def kernel(x, mem, mem_norm_g, mix_norm_g, xattn_norm_g, mlp_norm_g, s5_a_re, s5_a_im, s5_log_dt, s5_b_re, s5_b_im, s5_c_re, s5_c_im, s5_d, s5_w_glu, conv_w_in, conv_w, conv_w_out, xa_w_q, xa_w_kv, xa_w_o, mlp_w1, mlp_w2, final_norm_g):
    bsz, seq, dm = x.shape
    depth = mix_norm_g.shape[0]
    assert bsz == SUBLANES, "the S5 scan puts the batch on the sublanes of one tile"
    assert dm % MXU_TILE_V7X == 0
    t_scan, t_conv, t_attn, t_mlp = _tiles(seq)
    assert all(seq % t == 0 for t in (t_scan, t_conv, t_attn, t_mlp))

    mix_g, xattn_g, mlp_g = _gains(mix_norm_g), _gains(xattn_norm_g), _gains(mlp_norm_g)
    wb, a_rep, wc = _s5_discretise(s5_a_re, s5_a_im, s5_log_dt, s5_b_re, s5_b_im, s5_c_re, s5_c_im, bsz)
    s5_d3, w_glu = _gains(s5_d), s5_w_glu.astype(_BF16)
    w_in, w_out = conv_w_in.astype(_BF16), conv_w_out.astype(_BF16)
    w_q, w_o = xa_w_q.astype(_BF16), xa_w_o.astype(_BF16)
    w1, w2 = mlp_w1.astype(_BF16), mlp_w2.astype(_BF16)

    kv = _memory_kv(mem, mem_norm_g, xa_w_kv.astype(_BF16))
    qk, vo = _fold_memory(kv, w_q, w_o)
    for i in range(depth):
        j = i // 2
        if i % 2 == 0:
            x = _s5_layer(x, mix_g, i, wb, a_rep, wc, s5_d3, w_glu, j, t_scan)
        else:
            x = _conv_layer(x, mix_g, i, w_in, conv_w, w_out, j, t_conv)
        x = _xattn_layer(x, xattn_g, i, qk, vo, t_attn)
        x = _mlp_layer(x, mlp_g, i, w1, w2, t_mlp, final_g=final_norm_g if i == depth - 1 else None)
    return x
```
